```python
import math
import jax
import jax.numpy as jnp
from jax import lax
import numpy as np

D_MODEL = 1024
BATCH = 8
SEQ = 4096
DEPTH = 4

N_HEADS = 16
N_KV_HEADS = 4
HEAD_DIM = 64
GROUP = N_HEADS // N_KV_HEADS
WINDOW = 128
BLOCK = 128
ATTN_W = N_HEADS * HEAD_DIM
KV_W = N_KV_HEADS * HEAD_DIM
NUM_BUCKETS = 32
MAX_DISTANCE = 128
MAX_EXACT = NUM_BUCKETS // 2
D_RNN = D_MODEL
N_LRU_BLOCKS = 16
LRU_BLOCK = D_RNN // N_LRU_BLOCKS
CONV_WIDTH = 4
LRU_C = 8.0
D_FF = D_MODEL * 7 // 2
N_EXPERTS = 8
TOP_K = 2
N_DENSE = (DEPTH + 1) // 2
N_MOE = DEPTH // 2
PLE_DIM = 256
IN_SPLITS = (ATTN_W, KV_W, KV_W, D_RNN, D_RNN, D_MODEL, D_MODEL)
N_IN = sum(IN_SPLITS)
EPS = 1e-6
NEG_INF = -1e30

kernel_name = "hybrid_swa_rglru_moe_trunk"


def rmsnorm(x, g):
    xf = x.astype(jnp.float32)
    y = xf * lax.rsqrt(jnp.mean(xf * xf, axis=-1, keepdims=True) + EPS)
    return (y * g.astype(jnp.float32)).astype(x.dtype)


def t5_bucket(dist):
    d = jnp.maximum(dist, 0)
    large = MAX_EXACT + (jnp.log(jnp.maximum(d, 1).astype(jnp.float32) / MAX_EXACT)
                         / math.log(MAX_DISTANCE / MAX_EXACT)
                         * (NUM_BUCKETS - MAX_EXACT)).astype(jnp.int32)
    large = jnp.minimum(large, NUM_BUCKETS - 1)
    return jnp.where(d < MAX_EXACT, d, large)


def band_geometry(n_blocks):
    qi = jnp.arange(BLOCK, dtype=jnp.int32)[:, None]
    ks = jnp.arange(2 * BLOCK, dtype=jnp.int32)[None, :]
    dist = BLOCK + qi - ks
    in_window = (dist >= 0) & (dist < WINDOW)
    key_abs = jnp.arange(n_blocks, dtype=jnp.int32)[:, None, None] * BLOCK - BLOCK + ks[None]
    mask = in_window[None] & (key_abs >= 0)
    return t5_bucket(dist), mask


def head_rms(x, g):
    return rmsnorm(x, g)


def sliding_window_attention(q, k, v, bias, mask, sink):
    B, S = q.shape[0], q.shape[1]
    nb = S // BLOCK
    qb = q.reshape(B, nb, BLOCK, N_KV_HEADS, GROUP, HEAD_DIM)

    def band(t):
        tb = t.reshape(B, nb, BLOCK, N_KV_HEADS, HEAD_DIM)
        prev = jnp.pad(tb[:, :-1], ((0, 0), (1, 0), (0, 0), (0, 0), (0, 0)))
        return jnp.concatenate([prev, tb], axis=2)

    kb, vb = band(k), band(v)
    s = jnp.einsum('bnqkgd,bnskd->bnkgqs', qb, kb).astype(jnp.float32) * (HEAD_DIM ** -0.5)
    s = jnp.where(mask[None, :, None, None], s + bias[None, None].astype(jnp.float32), NEG_INF)
    sink_l = sink.astype(jnp.float32).reshape(1, 1, N_KV_HEADS, GROUP, 1, 1)
    m = jnp.maximum(jnp.max(s, axis=-1, keepdims=True), sink_l)
    e = jnp.exp(s - m)
    probs = e / (jnp.sum(e, axis=-1, keepdims=True) + jnp.exp(sink_l - m))
    o = jnp.einsum('bnkgqs,bnskd->bnqkgd', probs.astype(v.dtype), vb)
    return o.reshape(B, S, ATTN_W)


def block_diag(x, w, b):
    B, S, C = x.shape
    xb = x.reshape(B, S, N_LRU_BLOCKS, LRU_BLOCK)
    return (jnp.einsum('bsnc,ncd->bsnd', xb, w) + b).reshape(B, S, C)


def rg_lru_branch(xb, gb, conv_w, conv_b, w_rg, b_rg, w_ig, b_ig, lam):
    C = xb.shape[-1]
    xc = lax.conv_general_dilated(xb, conv_w[:, None, :], window_strides=(1,),
                                  padding=[(CONV_WIDTH - 1, 0)],
                                  dimension_numbers=('NWC', 'WIO', 'NWC'),
                                  feature_group_count=C) + conv_b
    r = jax.nn.sigmoid(block_diag(xc, w_rg, b_rg))
    i = jax.nn.sigmoid(block_diag(xc, w_ig, b_ig))
    log_a = -LRU_C * r.astype(jnp.float32) * jax.nn.softplus(-lam.astype(jnp.float32))
    a = jnp.exp(log_a)
    u = jnp.sqrt(-jnp.expm1(2.0 * log_a)) * (i * xc).astype(jnp.float32)

    def combine(left, right):
        a1, b1 = left
        a2, b2 = right
        return a1 * a2, a2 * b1 + b2

    _, h = lax.associative_scan(combine, (a, u), axis=1)
    return h.astype(xb.dtype) * jax.nn.gelu(gb)


def swiglu(x, w1, w3, w2):
    return (jax.nn.silu(x @ w1) * (x @ w3)) @ w2


def moe_ffn(xn, router_w, router_b, w1, w3, w2):
    logits = (xn @ router_w).astype(jnp.float32) + router_b.astype(jnp.float32)
    top_v, top_i = lax.top_k(logits, TOP_K)
    w = jax.nn.softmax(top_v, axis=-1)
    gate = jnp.sum(jax.nn.one_hot(top_i, N_EXPERTS, dtype=jnp.float32) * w[..., None], axis=-2)
    gate = gate.astype(xn.dtype)
    out = jnp.zeros_like(xn)
    for e in range(N_EXPERTS):
        out = out + gate[..., e:e + 1] * swiglu(xn, w1[e], w3[e], w2[e])
    return out


def setup_inputs(seed: int = 0) -> dict:
    key = jax.random.key(seed)
    keys = iter(jax.random.split(key, 40))
    f32 = jnp.float32

    def nrm(shape, fan_in):
        return jax.random.normal(next(keys), shape, f32) * (fan_in ** -0.5)

    def gain(shape):
        return 1.0 + 0.01 * jax.random.normal(next(keys), shape, f32)

    def small(shape, scale=0.01):
        return scale * jax.random.normal(next(keys), shape, f32)

    x = jax.random.normal(next(keys), (BATCH, SEQ, D_MODEL), f32)
    p = jax.random.normal(next(keys), (DEPTH, BATCH, SEQ, PLE_DIM), f32)
    mix_norm = gain((DEPTH, D_MODEL))
    w_in = nrm((DEPTH, D_MODEL, N_IN), D_MODEL)
    q_norm = gain((DEPTH, HEAD_DIM))
    k_norm = gain((DEPTH, HEAD_DIM))
    sinks = small((DEPTH, N_HEADS), 0.5)
    rel_bias = small((NUM_BUCKETS, N_HEADS), 0.5)
    conv_w = nrm((DEPTH, CONV_WIDTH, D_RNN), CONV_WIDTH)
    conv_b = small((DEPTH, D_RNN))
    w_rg = nrm((DEPTH, N_LRU_BLOCKS, LRU_BLOCK, LRU_BLOCK), LRU_BLOCK)
    b_rg = small((DEPTH, N_LRU_BLOCKS, LRU_BLOCK))
    w_ig = nrm((DEPTH, N_LRU_BLOCKS, LRU_BLOCK, LRU_BLOCK), LRU_BLOCK)
    b_ig = small((DEPTH, N_LRU_BLOCKS, LRU_BLOCK))
    a_c = jax.random.uniform(next(keys), (DEPTH, D_RNN), f32, 0.9, 0.999)
    sig_l = a_c ** (1.0 / LRU_C)
    lru_lambda = jnp.log(sig_l) - jnp.log1p(-sig_l)
    w_proj_attn = nrm((DEPTH, ATTN_W, D_MODEL), ATTN_W)
    w_proj_lru = nrm((DEPTH, D_RNN, D_MODEL), D_RNN)
    w_out = nrm((DEPTH, D_MODEL, D_MODEL), D_MODEL)
    ffn_norm = gain((DEPTH, D_MODEL))
    dense_w1 = nrm((N_DENSE, D_MODEL, D_FF), D_MODEL)
    dense_w3 = nrm((N_DENSE, D_MODEL, D_FF), D_MODEL)
    dense_w2 = nrm((N_DENSE, D_FF, D_MODEL), D_FF)
    router_w = nrm((N_MOE, D_MODEL, N_EXPERTS), D_MODEL)
    router_b = small((N_MOE, N_EXPERTS))
    moe_w1 = nrm((N_MOE, N_EXPERTS, D_MODEL, D_FF), D_MODEL)
    moe_w3 = nrm((N_MOE, N_EXPERTS, D_MODEL, D_FF), D_MODEL)
    moe_w2 = nrm((N_MOE, N_EXPERTS, D_FF, D_MODEL), D_FF)
    ple_norm = gain((DEPTH, D_MODEL))
    ple_gate_w = nrm((DEPTH, D_MODEL, D_MODEL), D_MODEL)
    ple_proj_w = nrm((DEPTH, PLE_DIM, D_MODEL), PLE_DIM)
    return {"x": x, "p": p, "mix_norm": mix_norm, "w_in": w_in, "q_norm": q_norm,
            "k_norm": k_norm, "sinks": sinks, "rel_bias": rel_bias, "conv_w": conv_w,
            "conv_b": conv_b, "w_rg": w_rg, "b_rg": b_rg, "w_ig": w_ig, "b_ig": b_ig,
            "lru_lambda": lru_lambda, "w_proj_attn": w_proj_attn, "w_proj_lru": w_proj_lru,
            "w_out": w_out, "ffn_norm": ffn_norm, "dense_w1": dense_w1, "dense_w3": dense_w3,
            "dense_w2": dense_w2, "router_w": router_w, "router_b": router_b,
            "moe_w1": moe_w1, "moe_w3": moe_w3, "moe_w2": moe_w2, "ple_norm": ple_norm,
            "ple_gate_w": ple_gate_w, "ple_proj_w": ple_proj_w}


def reference(x, p, mix_norm, w_in, q_norm, k_norm, sinks, rel_bias, conv_w, conv_b,
              w_rg, b_rg, w_ig, b_ig, lru_lambda, w_proj_attn, w_proj_lru, w_out,
              ffn_norm, dense_w1, dense_w3, dense_w2, router_w, router_b,
              moe_w1, moe_w3, moe_w2, ple_norm, ple_gate_w, ple_proj_w):
    B, S, _ = x.shape
    nb = S // BLOCK
    bucket, mask = band_geometry(nb)
    bias = jnp.transpose(rel_bias[bucket], (2, 0, 1)).reshape(N_KV_HEADS, GROUP, BLOCK, 2 * BLOCK)
    split_points = list(np.cumsum(IN_SPLITS)[:-1])

    h = x
    for i in range(DEPTH):
        xn = rmsnorm(h, mix_norm[i])
        u = xn @ w_in[i]
        q, k, v, lx, lg, ga, gb = jnp.split(u, split_points, axis=-1)
        q = head_rms(q.reshape(B, S, N_HEADS, HEAD_DIM), q_norm[i])
        k = head_rms(k.reshape(B, S, N_KV_HEADS, HEAD_DIM), k_norm[i])
        v = v.reshape(B, S, N_KV_HEADS, HEAD_DIM)
        y_a = sliding_window_attention(q, k, v, bias, mask, sinks[i])
        y_b = rg_lru_branch(lx, lg, conv_w[i], conv_b[i], w_rg[i], b_rg[i],
                            w_ig[i], b_ig[i], lru_lambda[i])
        merged = (jax.nn.sigmoid(ga) * (y_a @ w_proj_attn[i])
                  + jax.nn.sigmoid(gb) * (y_b @ w_proj_lru[i]))
        h = h + merged @ w_out[i]
        xn2 = rmsnorm(h, ffn_norm[i])
        j = i // 2
        if i % 2 == 0:
            h = h + swiglu(xn2, dense_w1[j], dense_w3[j], dense_w2[j])
        else:
            h = h + moe_ffn(xn2, router_w[j], router_b[j], moe_w1[j], moe_w3[j], moe_w2[j])
        xn3 = rmsnorm(h, ple_norm[i])
        h = h + jax.nn.sigmoid(xn3 @ ple_gate_w[i]) * (p[i] @ ple_proj_w[i])
    return h
```

```python
import functools
import math

import jax
import jax.numpy as jnp
from jax import lax
from jax.experimental import pallas as pl
from jax.experimental.pallas import tpu as pltpu

N_HEADS = 16
N_KV_HEADS = 4
HEAD_DIM = 64
GROUP = N_HEADS // N_KV_HEADS
WINDOW = 128
BLOCK = 128
NUM_BUCKETS = 32
MAX_DISTANCE = 128
MAX_EXACT = NUM_BUCKETS // 2
N_LRU_BLOCKS = 16
CONV_WIDTH = 4
LRU_C = 8.0
N_EXPERTS = 8
TOP_K = 2
EPS = 1e-6
NEG_INF = -1e30

V7X_LANES = 128
V7X_SUBLANES = 8
V7X_MXU_DIM = 256
V7X_VMEM_LIMIT_BYTES = 56 * 1024 * 1024

BF16 = jnp.bfloat16
F32 = jnp.float32


def _cparams(*sem):
    return pltpu.CompilerParams(dimension_semantics=sem,
                                vmem_limit_bytes=V7X_VMEM_LIMIT_BYTES)


def _tile(n, pref):
    t = min(n, pref)
    while n % t:
        t //= 2
    return t


def _rms(x, g):
    ms = jnp.mean(x * x, axis=-1, keepdims=True)
    return (x * lax.rsqrt(ms + EPS)) * g


def _inproj_body(h_ref, g_ref, w_ref, o_ref, xn_ref):
    @pl.when(pl.program_id(1) == 0)
    def _():
        xn_ref[...] = _rms(h_ref[...], g_ref[...]).astype(BF16)

    o_ref[...] = jnp.dot(xn_ref[...], w_ref[...],
                         preferred_element_type=F32).astype(o_ref.dtype)


def _inproj(h, g, w):
    m, d = h.shape
    n = w.shape[1]
    tm = _tile(m, 1024)
    tn = _tile(n, 512)
    return pl.pallas_call(
        _inproj_body,
        grid=(m // tm, n // tn),
        in_specs=[pl.BlockSpec((tm, d), lambda i, j: (i, 0)),
                  pl.BlockSpec((1, d), lambda i, j: (0, 0)),
                  pl.BlockSpec((d, tn), lambda i, j: (0, j))],
        out_specs=pl.BlockSpec((tm, tn), lambda i, j: (i, j)),
        out_shape=jax.ShapeDtypeStruct((m, n), BF16),
        scratch_shapes=[pltpu.VMEM((tm, d), BF16)],
        compiler_params=_cparams("parallel", "arbitrary"),
        name="inproj",
    )(h, g, w)


def _head_rms(x, g):
    ms = jnp.mean(x * x, axis=-1, keepdims=True)
    return (x * lax.rsqrt(ms + EPS)) * g


def _attn_body(sink_ref, q_ref, kp_ref, ko_ref, vp_ref, vo_ref, bias_ref, gq_ref, gk_ref, o_ref):
    n = pl.program_id(1)
    q = q_ref[0]
    kb = jnp.concatenate([kp_ref[0], ko_ref[0]], axis=0)
    vb = jnp.concatenate([vp_ref[0], vo_ref[0]], axis=0)
    qi = lax.broadcasted_iota(jnp.int32, (BLOCK, 2 * BLOCK), 0)
    ks = lax.broadcasted_iota(jnp.int32, (BLOCK, 2 * BLOCK), 1)
    dist = BLOCK + qi - ks
    mask = (dist >= 0) & (dist < WINDOW) & ((ks >= BLOCK) | (n > 0))
    gq = gq_ref[...]
    gk = gk_ref[...]
    scale = HEAD_DIM ** -0.5
    outs = []
    for kv in range(N_KV_HEADS):
        k_h = kb[:, kv * HEAD_DIM:(kv + 1) * HEAD_DIM].astype(F32)
        k_h = _head_rms(k_h, gk).astype(BF16)
        v_h = vb[:, kv * HEAD_DIM:(kv + 1) * HEAD_DIM]
        for g in range(GROUP):
            hd = kv * GROUP + g
            q_h = q[:, hd * HEAD_DIM:(hd + 1) * HEAD_DIM].astype(F32)
            q_h = _head_rms(q_h, gq).astype(BF16)
            s = lax.dot_general(q_h, k_h, (((1,), (1,)), ((), ())),
                                preferred_element_type=F32) * scale
            s = jnp.where(mask, s + bias_ref[hd], NEG_INF)
            sink = sink_ref[hd]
            mx = jnp.maximum(jnp.max(s, axis=-1, keepdims=True), sink)
            e = jnp.exp(s - mx)
            denom = jnp.sum(e, axis=-1, keepdims=True) + jnp.exp(sink - mx)
            p = (e / denom).astype(BF16)
            outs.append(jnp.dot(p, v_h, preferred_element_type=F32))
    o_ref[0] = jnp.concatenate(outs, axis=-1).astype(o_ref.dtype)


def _attention(u3, bias, sinks, gq, gk, col_q, col_k, col_v):
    b, s, _ = u3.shape
    nb = s // BLOCK
    aw = N_HEADS * HEAD_DIM
    kw = N_KV_HEADS * HEAD_DIM
    grid_spec = pltpu.PrefetchScalarGridSpec(
        num_scalar_prefetch=0,
        grid=(b, nb),
        in_specs=[
            pl.BlockSpec(memory_space=pltpu.SMEM),
            pl.BlockSpec((1, BLOCK, aw), lambda i, j: (i, j, col_q // aw)),
            pl.BlockSpec((1, BLOCK, kw), lambda i, j: (i, jnp.maximum(j - 1, 0), col_k // kw)),
            pl.BlockSpec((1, BLOCK, kw), lambda i, j: (i, j, col_k // kw)),
            pl.BlockSpec((1, BLOCK, kw), lambda i, j: (i, jnp.maximum(j - 1, 0), col_v // kw)),
            pl.BlockSpec((1, BLOCK, kw), lambda i, j: (i, j, col_v // kw)),
            pl.BlockSpec((N_HEADS, BLOCK, 2 * BLOCK), lambda i, j: (0, 0, 0)),
            pl.BlockSpec((1, HEAD_DIM), lambda i, j: (0, 0)),
            pl.BlockSpec((1, HEAD_DIM), lambda i, j: (0, 0)),
        ],
        out_specs=pl.BlockSpec((1, BLOCK, aw), lambda i, j: (i, j, 0)),
    )
    return pl.pallas_call(
        _attn_body,
        grid_spec=grid_spec,
        out_shape=jax.ShapeDtypeStruct((b, s, aw), BF16),
        compiler_params=_cparams("parallel", "parallel"),
        name="swa_attention",
    )(sinks, u3, u3, u3, u3, u3, bias, gq, gk)


LRU_SUB = 128
LRU_GATE_W = V7X_MXU_DIM


def _lru_sub_step(x_bf, lg_bf, hist, hcarry, cw_ref, cb_ref, wg_ref, brg_ref, big_ref, sp):
    rows, cw = x_bf.shape
    x = x_bf.astype(F32)
    r8 = lax.broadcasted_iota(jnp.int32, (V7X_SUBLANES, cw), 0)
    xc = x * cw_ref[CONV_WIDTH - 1:CONV_WIDTH, :] + cb_ref[...]
    for k in range(1, CONV_WIDTH):
        xs = pltpu.roll(x, k, axis=0)
        hs = pltpu.roll(hist, k, axis=0)
        top = jnp.where(r8 < k, hs, xs[0:V7X_SUBLANES])
        xs = jnp.concatenate([top, xs[V7X_SUBLANES:]], axis=0)
        xc = xc + xs * cw_ref[CONV_WIDTH - 1 - k:CONV_WIDTH - k, :]
    new_hist = x[rows - V7X_SUBLANES:rows]

    xcb = xc.astype(BF16)
    r_parts, i_parts = [], []
    for c in range(cw // LRU_GATE_W):
        pre = jnp.dot(xcb[:, c * LRU_GATE_W:(c + 1) * LRU_GATE_W], wg_ref[c],
                      preferred_element_type=F32)
        r_parts.append(pre[:, :LRU_GATE_W])
        i_parts.append(pre[:, LRU_GATE_W:])
    r = jax.nn.sigmoid(jnp.concatenate(r_parts, axis=1) + brg_ref[...])
    ig = jax.nn.sigmoid(jnp.concatenate(i_parts, axis=1) + big_ref[...])
    log_a = (-LRU_C * r) * sp
    a = jnp.exp(log_a)
    u = jnp.sqrt(jnp.tanh(-log_a) * (1.0 + a * a)) * (ig * xc)

    ng = rows // V7X_SUBLANES
    a3 = a.reshape(ng, V7X_SUBLANES, cw)
    u3 = u.reshape(ng, V7X_SUBLANES, cw)
    sub = lax.broadcasted_iota(jnp.int32, (ng, V7X_SUBLANES, cw), 1)
    k = 1
    while k < V7X_SUBLANES:
        a_sh = pltpu.roll(a3, k, axis=1)
        u_sh = pltpu.roll(u3, k, axis=1)
        valid = sub >= k
        u3 = jnp.where(valid, u3 + a3 * u_sh, u3)
        a3 = jnp.where(valid, a3 * a_sh, a3)
        k *= 2
    hs_out = []
    carry = hcarry
    for m in range(ng):
        hm = u3[m] + a3[m] * carry
        carry = hm[V7X_SUBLANES - 1:V7X_SUBLANES, :]
        hs_out.append(hm)
    h = jnp.concatenate(hs_out, axis=0)
    y = h * jax.nn.gelu(lg_bf.astype(F32))
    return y, new_hist, carry


def _lru_body(x_ref, lg_ref, cw_ref, cb_ref, wg_ref, brg_ref, big_ref, lam_ref, o_ref,
              hist_ref, carry_ref):
    @pl.when(pl.program_id(2) == 0)
    def _():
        hist_ref[...] = jnp.zeros_like(hist_ref)
        carry_ref[...] = jnp.zeros_like(carry_ref)

    sp = jax.nn.softplus(-lam_ref[...])
    tc = x_ref.shape[1]

    def step(i, c):
        r0 = pl.multiple_of(i * LRU_SUB, LRU_SUB)
        y, new_hist, carry = _lru_sub_step(
            x_ref[0, pl.ds(r0, LRU_SUB), :], lg_ref[0, pl.ds(r0, LRU_SUB), :],
            hist_ref[...], carry_ref[...], cw_ref, cb_ref, wg_ref, brg_ref, big_ref, sp)
        o_ref[0, pl.ds(r0, LRU_SUB), :] = y.astype(o_ref.dtype)
        hist_ref[...] = new_hist
        carry_ref[...] = carry
        return c

    lax.fori_loop(0, tc // LRU_SUB, step, 0)


def _lru(u3, conv_w, conv_b, wg, brg, big, lam, col_x, col_g):
    b, s, _ = u3.shape
    c = conv_w.shape[1]
    cw = 512
    tc = _tile(s, 512)
    ngate = cw // LRU_GATE_W
    return pl.pallas_call(
        _lru_body,
        grid=(b, c // cw, s // tc),
        in_specs=[
            pl.BlockSpec((1, tc, cw), lambda i, j, t: (i, t, col_x // cw + j)),
            pl.BlockSpec((1, tc, cw), lambda i, j, t: (i, t, col_g // cw + j)),
            pl.BlockSpec((CONV_WIDTH, cw), lambda i, j, t: (0, j)),
            pl.BlockSpec((1, cw), lambda i, j, t: (0, j)),
            pl.BlockSpec((ngate, LRU_GATE_W, 2 * LRU_GATE_W), lambda i, j, t: (j, 0, 0)),
            pl.BlockSpec((1, cw), lambda i, j, t: (0, j)),
            pl.BlockSpec((1, cw), lambda i, j, t: (0, j)),
            pl.BlockSpec((1, cw), lambda i, j, t: (0, j)),
        ],
        out_specs=pl.BlockSpec((1, tc, cw), lambda i, j, t: (i, t, j)),
        out_shape=jax.ShapeDtypeStruct((b, s, c), BF16),
        scratch_shapes=[pltpu.VMEM((V7X_SUBLANES, cw), F32),
                        pltpu.VMEM((1, cw), F32)],
        compiler_params=_cparams("parallel", "parallel", "arbitrary"),
        name="rg_lru",
    )(u3, u3, conv_w, conv_b, wg, brg, big, lam)


def _merge_body(ya_ref, yb_ref, ga_ref, gb_ref, h_ref, wpa_ref, wpb_ref, wo_ref, o_ref):
    pa = jnp.dot(ya_ref[...], wpa_ref[...], preferred_element_type=F32)
    pb = jnp.dot(yb_ref[...], wpb_ref[...], preferred_element_type=F32)
    m = (jax.nn.sigmoid(ga_ref[...].astype(F32)) * pa
         + jax.nn.sigmoid(gb_ref[...].astype(F32)) * pb)
    o_ref[...] = h_ref[...] + jnp.dot(m.astype(BF16), wo_ref[...], preferred_element_type=F32)


def _merge(ya, yb, u2, h, wpa, wpb, wo, col_ga, col_gb):
    m, d = h.shape
    tm = _tile(m, 512)
    row = lambda i: (i, 0)
    full = lambda i: (0, 0)
    return pl.pallas_call(
        _merge_body,
        grid=(m // tm,),
        in_specs=[pl.BlockSpec((tm, d), row),
                  pl.BlockSpec((tm, d), row),
                  pl.BlockSpec((tm, d), lambda i: (i, col_ga // d)),
                  pl.BlockSpec((tm, d), lambda i: (i, col_gb // d)),
                  pl.BlockSpec((tm, d), row),
                  pl.BlockSpec(wpa.shape, full),
                  pl.BlockSpec(wpb.shape, full),
                  pl.BlockSpec(wo.shape, full)],
        out_specs=pl.BlockSpec((tm, d), row),
        out_shape=jax.ShapeDtypeStruct((m, d), F32),
        compiler_params=_cparams("parallel"),
        name="merge_outproj",
    )(ya, yb, u2, u2, h, wpa, wpb, wo)


def _swiglu_accumulate(x, w1_ref, w3_ref, w2_ref, acc_ref):
    a = jnp.dot(x, w1_ref[0], preferred_element_type=F32)
    b = jnp.dot(x, w3_ref[0], preferred_element_type=F32)
    hmid = (jax.nn.silu(a) * b).astype(BF16)
    acc_ref[...] += jnp.dot(hmid, w2_ref[0], preferred_element_type=F32)


def _ffn_dense_body(h_ref, g_ref, w1_ref, w3_ref, w2_ref, o_ref, xn_ref, acc_ref):
    j = pl.program_id(1)

    @pl.when(j == 0)
    def _():
        xn_ref[...] = _rms(h_ref[...], g_ref[...]).astype(BF16)
        acc_ref[...] = jnp.zeros_like(acc_ref)

    _swiglu_accumulate(xn_ref[...], w1_ref, w3_ref, w2_ref, acc_ref)

    @pl.when(j == pl.num_programs(1) - 1)
    def _():
        o_ref[...] = h_ref[...] + acc_ref[...]


def _ffn_dense(h, g, w1, w3, w2):
    m, d = h.shape
    f = w1.shape[2]
    tm = _tile(m, 1024)
    tf = 512
    return pl.pallas_call(
        _ffn_dense_body,
        grid=(m // tm, f // tf),
        in_specs=[pl.BlockSpec((tm, d), lambda i, j: (i, 0)),
                  pl.BlockSpec((1, d), lambda i, j: (0, 0)),
                  pl.BlockSpec((1, d, tf), lambda i, j: (0, 0, j)),
                  pl.BlockSpec((1, d, tf), lambda i, j: (0, 0, j)),
                  pl.BlockSpec((1, tf, d), lambda i, j: (0, j, 0))],
        out_specs=pl.BlockSpec((tm, d), lambda i, j: (i, 0)),
        out_shape=jax.ShapeDtypeStruct((m, d), F32),
        scratch_shapes=[pltpu.VMEM((tm, d), BF16), pltpu.VMEM((tm, d), F32)],
        compiler_params=_cparams("parallel", "arbitrary"),
        name="ffn_dense",
    )(h, g, w1, w3, w2)


def _ffn_moe_body(te_ref, na_ref, x_ref, w1_ref, w3_ref, w2_ref, o_ref, xb_ref, acc_ref):
    i = pl.program_id(0)
    j = pl.program_id(1)
    active = i < na_ref[0]

    @pl.when(active & (j == 0))
    def _():
        xb_ref[...] = x_ref[...].astype(BF16)
        acc_ref[...] = jnp.zeros_like(acc_ref)

    @pl.when(active)
    def _():
        _swiglu_accumulate(xb_ref[...], w1_ref, w3_ref, w2_ref, acc_ref)

    @pl.when(j == pl.num_programs(1) - 1)
    def _():
        o_ref[...] = jnp.where(active, acc_ref[...], 0.0)


def _ffn_moe(xs, tile_expert, n_active, w1, w3, w2, tm):
    r, d = xs.shape
    f = w1.shape[2]
    tf = 512
    nt = r // tm

    def xmap(i, j, te, na):
        return (jnp.minimum(i, na[0] - 1), 0)

    def w13map(i, j, te, na):
        return (te[i], 0, jnp.where(i < na[0], j, f // tf - 1))

    def w2map(i, j, te, na):
        return (te[i], jnp.where(i < na[0], j, f // tf - 1), 0)

    grid_spec = pltpu.PrefetchScalarGridSpec(
        num_scalar_prefetch=2,
        grid=(nt, f // tf),
        in_specs=[pl.BlockSpec((tm, d), xmap),
                  pl.BlockSpec((1, d, tf), w13map),
                  pl.BlockSpec((1, d, tf), w13map),
                  pl.BlockSpec((1, tf, d), w2map)],
        out_specs=pl.BlockSpec((tm, d), lambda i, j, te, na: (i, 0)),
        scratch_shapes=[pltpu.VMEM((tm, d), BF16), pltpu.VMEM((tm, d), F32)],
    )
    return pl.pallas_call(
        _ffn_moe_body,
        grid_spec=grid_spec,
        out_shape=jax.ShapeDtypeStruct((r, d), F32),
        compiler_params=_cparams("arbitrary", "arbitrary"),
        name="ffn_moe_grouped",
    )(tile_expert, n_active, xs, w1, w3, w2)


def _router_body(h_ref, g_ref, rw_ref, rb_ref, idx_ref, wts_ref):
    xn = _rms(h_ref[...], g_ref[...])
    logits = jnp.dot(xn, rw_ref[...], preferred_element_type=F32,
                     precision=lax.Precision.HIGHEST) + rb_ref[...]
    lanes = lax.broadcasted_iota(jnp.int32, logits.shape, 1)
    lg = jnp.where(lanes < N_EXPERTS, logits, -jnp.inf)
    v1 = jnp.max(lg, axis=-1, keepdims=True)
    i1 = jnp.min(jnp.where(lg == v1, lanes, V7X_LANES), axis=-1, keepdims=True)
    lg2 = jnp.where(lanes == i1, -jnp.inf, lg)
    v2 = jnp.max(lg2, axis=-1, keepdims=True)
    i2 = jnp.min(jnp.where(lg2 == v2, lanes, V7X_LANES), axis=-1, keepdims=True)
    e2 = jnp.exp(v2 - v1)
    w1 = 1.0 / (1.0 + e2)
    w2 = e2 / (1.0 + e2)
    idx_ref[...] = jnp.where(lanes == 0, i1, jnp.where(lanes == 1, i2, 0))
    wts_ref[...] = jnp.where(lanes == 0, w1, jnp.where(lanes == 1, w2, 0.0))


def _router(h, g, rw, rb):
    m, d = h.shape
    tm = _tile(m, 512)
    row = lambda i: (i, 0)
    full = lambda i: (0, 0)
    return pl.pallas_call(
        _router_body,
        grid=(m // tm,),
        in_specs=[pl.BlockSpec((tm, d), row),
                  pl.BlockSpec((1, d), full),
                  pl.BlockSpec((d, V7X_LANES), full),
                  pl.BlockSpec((1, V7X_LANES), full)],
        out_specs=[pl.BlockSpec((tm, V7X_LANES), row),
                   pl.BlockSpec((tm, V7X_LANES), row)],
        out_shape=[jax.ShapeDtypeStruct((m, V7X_LANES), jnp.int32),
                   jax.ShapeDtypeStruct((m, V7X_LANES), F32)],
        compiler_params=_cparams("parallel"),
        name="moe_router",
    )(h, g, rw, rb)


def _row_copy(src_ref, src_row, dst_ref, dst_row, sem):
    return pltpu.make_async_copy(src_ref.at[pl.ds(src_row, 1)],
                                 dst_ref.at[pl.ds(dst_row, 1)], sem)


def _dispatch_body(pos_ref, h_ref, g_ref, xs_in_ref, xs_ref, xn_ref, sem):
    del xs_in_ref
    xn_ref[...] = _rms(h_ref[...], g_ref[...])
    td = xn_ref.shape[0]

    def issue(r, c):
        for k in range(TOP_K):
            _row_copy(xn_ref, r, xs_ref, pos_ref[0, 0, TOP_K * r + k], sem).start()
        return c

    lax.fori_loop(0, td, issue, 0)

    def drain(r, c):
        for k in range(TOP_K):
            _row_copy(xn_ref, r, xs_ref, pos_ref[0, 0, TOP_K * r + k], sem).wait()
        return c

    lax.fori_loop(0, td, drain, 0)


def _dispatch(pos3, h, g, xs0):
    m, d = h.shape
    td = pos3.shape[2] // TOP_K
    return pl.pallas_call(
        _dispatch_body,
        grid=(m // td,),
        in_specs=[pl.BlockSpec((1, 1, TOP_K * td), lambda i: (i, 0, 0), memory_space=pltpu.SMEM),
                  pl.BlockSpec((td, d), lambda i: (i, 0)),
                  pl.BlockSpec((1, d), lambda i: (0, 0)),
                  pl.BlockSpec(memory_space=pl.ANY)],
        out_specs=pl.BlockSpec(memory_space=pl.ANY),
        out_shape=jax.ShapeDtypeStruct(xs0.shape, xs0.dtype),
        scratch_shapes=[pltpu.VMEM((td, d), F32), pltpu.SemaphoreType.DMA(())],
        input_output_aliases={3: 0},
        compiler_params=_cparams("arbitrary"),
        name="moe_dispatch",
    )(pos3, h, g, xs0)


def _combine_body(pos_ref, h_ref, wts_ref, ys_ref, o_ref, buf_ref, sem):
    tc = h_ref.shape[0]

    def issue(r, c):
        for k in range(TOP_K):
            _row_copy(ys_ref, pos_ref[0, 0, TOP_K * r + k], buf_ref.at[k], r, sem).start()
        return c

    lax.fori_loop(0, tc, issue, 0)

    def drain(r, c):
        for k in range(TOP_K):
            _row_copy(ys_ref, pos_ref[0, 0, TOP_K * r + k], buf_ref.at[k], r, sem).wait()
        return c

    lax.fori_loop(0, tc, drain, 0)
    w = wts_ref[...]
    o_ref[...] = h_ref[...] + w[:, 0:1] * buf_ref[0] + w[:, 1:2] * buf_ref[1]


def _combine(pos3, h, wts, ys):
    m, d = h.shape
    tc = pos3.shape[2] // TOP_K
    return pl.pallas_call(
        _combine_body,
        grid=(m // tc,),
        in_specs=[pl.BlockSpec((1, 1, TOP_K * tc), lambda i: (i, 0, 0), memory_space=pltpu.SMEM),
                  pl.BlockSpec((tc, d), lambda i: (i, 0)),
                  pl.BlockSpec((tc, V7X_LANES), lambda i: (i, 0)),
                  pl.BlockSpec(memory_space=pl.ANY)],
        out_specs=pl.BlockSpec((tc, d), lambda i: (i, 0)),
        out_shape=jax.ShapeDtypeStruct((m, d), F32),
        scratch_shapes=[pltpu.VMEM((TOP_K, tc, d), F32), pltpu.SemaphoreType.DMA(())],
        compiler_params=_cparams("arbitrary"),
        name="moe_combine",
    )(pos3, h, wts, ys)


MOE_ROW_TILE = 1024
MOE_TOKEN_TILE = 256


def _moe_plan(idx, m, tm):
    e = idx[:, :TOP_K].reshape(-1)
    onehot = (e[:, None] == jnp.arange(N_EXPERTS, dtype=jnp.int32)[None, :]).astype(jnp.int32)
    csum = jnp.cumsum(onehot, axis=0)
    rank = jnp.sum(onehot * (csum - 1), axis=1)
    counts = csum[-1]
    ntile_e = (counts + tm - 1) // tm
    tile_end = jnp.cumsum(ntile_e)
    poff = (tile_end - ntile_e) * tm
    pos = jnp.sum(onehot * poff[None, :], axis=1) + rank
    n_active = tile_end[-1:]
    nt = (TOP_K * m) // tm + N_EXPERTS
    tiles = jnp.arange(nt, dtype=jnp.int32)
    te = jnp.sum((tiles[:, None] >= tile_end[None, :]).astype(jnp.int32), axis=1)
    te_last = jnp.sum((n_active - 1 >= tile_end).astype(jnp.int32))
    te = jnp.where(tiles < n_active, te, te_last)
    return pos.astype(jnp.int32), te.astype(jnp.int32), n_active.astype(jnp.int32), nt


def _moe_ffn(h, g, rw, rb, w1, w3, w2):
    m, d = h.shape
    tm = _tile(TOP_K * m, MOE_ROW_TILE)
    td = _tile(m, MOE_TOKEN_TILE)
    idx, wts = _router(h, g, rw, rb)
    pos, te, n_active, nt = _moe_plan(idx, m, tm)
    pos3 = pos.reshape(m // td, 1, TOP_K * td)
    xs0 = jnp.zeros((nt * tm, d), F32)
    xs = _dispatch(pos3, h, g, xs0)
    ys = _ffn_moe(xs, te, n_active, w1, w3, w2, tm)
    return _combine(pos3, h, wts, ys)


def _ple_body(h_ref, p_ref, g_ref, wg_ref, wp_ref, o_ref):
    h = h_ref[...]
    xn = _rms(h, g_ref[...]).astype(BF16)
    gate = jax.nn.sigmoid(jnp.dot(xn, wg_ref[...], preferred_element_type=F32))
    proj = jnp.dot(p_ref[...].astype(BF16), wp_ref[...], preferred_element_type=F32)
    o_ref[...] = h + gate * proj


def _ple(h, p, g, wg, wp):
    m, d = h.shape
    pd = p.shape[1]
    tm = _tile(m, 512)
    row = lambda i: (i, 0)
    full = lambda i: (0, 0)
    return pl.pallas_call(
        _ple_body,
        grid=(m // tm,),
        in_specs=[pl.BlockSpec((tm, d), row),
                  pl.BlockSpec((tm, pd), row),
                  pl.BlockSpec((1, d), full),
                  pl.BlockSpec(wg.shape, full),
                  pl.BlockSpec(wp.shape, full)],
        out_specs=pl.BlockSpec((tm, d), row),
        out_shape=jax.ShapeDtypeStruct((m, d), F32),
        compiler_params=_cparams("parallel"),
        name="ple",
    )(h, p, g, wg, wp)


def _t5_bias_table(rel_bias):
    qi = jnp.arange(BLOCK, dtype=jnp.int32)[:, None]
    ks = jnp.arange(2 * BLOCK, dtype=jnp.int32)[None, :]
    dist = jnp.maximum(BLOCK + qi - ks, 0)
    large = MAX_EXACT + (jnp.log(jnp.maximum(dist, 1).astype(F32) / MAX_EXACT)
                         / math.log(MAX_DISTANCE / MAX_EXACT)
                         * (NUM_BUCKETS - MAX_EXACT)).astype(jnp.int32)
    large = jnp.minimum(large, NUM_BUCKETS - 1)
    bucket = jnp.where(dist < MAX_EXACT, dist, large)
    return jnp.transpose(rel_bias[bucket], (2, 0, 1)).astype(F32)


def _gate_blockdiag(w_rg, w_ig):
    nblk, bs, _ = w_rg.shape
    per = LRU_GATE_W // bs
    eye = jnp.eye(per, dtype=w_rg.dtype)

    def bd(w):
        w4 = w.reshape(nblk // per, per, bs, bs)
        return jnp.einsum('ciab,ij->ciajb', w4, eye).reshape(nblk // per, LRU_GATE_W, LRU_GATE_W)

    return jnp.concatenate([bd(w_rg), bd(w_ig)], axis=2).astype(BF16)


def kernel(x, p, mix_norm, w_in, q_norm, k_norm, sinks, rel_bias, conv_w, conv_b, w_rg, b_rg, w_ig, b_ig, lru_lambda, w_proj_attn, w_proj_lru, w_out, ffn_norm, dense_w1, dense_w3, dense_w2, router_w, router_b, moe_w1, moe_w3, moe_w2, ple_norm, ple_gate_w, ple_proj_w):
    b, s, d = x.shape
    depth = w_in.shape[0]
    m = b * s
    aw = N_HEADS * HEAD_DIM
    kw = N_KV_HEADS * HEAD_DIM
    c = conv_w.shape[2]
    col_q, col_x, col_g, col_ga, col_gb = 0, aw, aw + c, aw + 2 * c, aw + 2 * c + d
    col_k = aw + 2 * c + 2 * d
    col_v = col_k + kw
    perm = jnp.concatenate([jnp.arange(0, aw), jnp.arange(aw + 2 * kw, w_in.shape[2]),
                            jnp.arange(aw, aw + 2 * kw)])
    bias = _t5_bias_table(rel_bias)

    h = x.reshape(m, d)
    for i in range(depth):
        w_in_i = w_in[i][:, perm].astype(BF16)
        u2 = _inproj(h, mix_norm[i][None, :], w_in_i)
        u3 = u2.reshape(b, s, u2.shape[1])
        y_a = _attention(u3, bias, sinks[i], q_norm[i][None, :], k_norm[i][None, :],
                         col_q, col_k, col_v)
        y_b = _lru(u3, conv_w[i], conv_b[i][None, :], _gate_blockdiag(w_rg[i], w_ig[i]),
                   b_rg[i].reshape(1, c), b_ig[i].reshape(1, c), lru_lambda[i][None, :],
                   col_x, col_g)
        h = _merge(y_a.reshape(m, aw), y_b.reshape(m, c), u2, h,
                   w_proj_attn[i].astype(BF16), w_proj_lru[i].astype(BF16),
                   w_out[i].astype(BF16), col_ga, col_gb)
        j = i // 2
        g_ffn = ffn_norm[i][None, :]
        if i % 2 == 0:
            h = _ffn_dense(h, g_ffn, dense_w1[j][None].astype(BF16),
                           dense_w3[j][None].astype(BF16), dense_w2[j][None].astype(BF16))
        else:
            rw = jnp.pad(router_w[j], ((0, 0), (0, V7X_LANES - N_EXPERTS)))
            rb = jnp.pad(router_b[j], (0, V7X_LANES - N_EXPERTS))[None, :]
            h = _moe_ffn(h, g_ffn, rw, rb, moe_w1[j].astype(BF16), moe_w3[j].astype(BF16),
                         moe_w2[j].astype(BF16))
        h = _ple(h, p[i].reshape(m, p.shape[3]), ple_norm[i][None, :],
                 ple_gate_w[i].astype(BF16), ple_proj_w[i].astype(BF16))
    return h.reshape(b, s, d)
```
